```python
import numpy as np
import jax, jax.numpy as jnp
from jax import lax

D_MODEL = 2048
BATCH = 2
SEQ = 4096
DEPTH = 2

GRID_W = 64
EPS = 1e-6
NEG_INF = -1e30

NA_HEADS = 8
NA_HEAD_DIM = 128
NA_WIDTH = NA_HEADS * NA_HEAD_DIM
NA_ROWS = 8
NA_COLS = 16
NA_COL_BLOCK = 16
NA_COL_SPAN = NA_COL_BLOCK + NA_COLS

SGU_GROUPS = 8
SGU_GROUP_DIM = 128
SGU_WIDTH = SGU_GROUPS * SGU_GROUP_DIM
SGU_CHUNK = 128

MLA_HEADS = 8
MLA_Q_RANK = 512
MLA_KV_RANK = 512
MLA_NOPE = 128
MLA_ROPE = 64
MLA_V = 128
MLA_WIDTH = MLA_HEADS * MLA_V
MLA_Q_BLOCK = 128
ROPE_THETA = 10000.0

N_BRANCH = 3
IN_SPLITS = [NA_WIDTH, 2 * NA_WIDTH, 3 * NA_WIDTH, 3 * NA_WIDTH + 2 * SGU_WIDTH,
             3 * NA_WIDTH + 2 * SGU_WIDTH + MLA_Q_RANK,
             3 * NA_WIDTH + 2 * SGU_WIDTH + MLA_Q_RANK + MLA_KV_RANK]
IN_COLS = 3 * NA_WIDTH + 2 * SGU_WIDTH + MLA_Q_RANK + MLA_KV_RANK + MLA_ROPE

PEER_HEADS = 8
PEER_N_KEYS = 128
PEER_N_EXPERTS = PEER_N_KEYS * PEER_N_KEYS
PEER_KEY_HALF = 128
PEER_TOPK = 16
PEER_TOKEN_BLOCK = 128

kernel_name = "hybrid_natten_sgu_mla_peer_encoder"


def rms_norm(x, g):
    xf = x.astype(jnp.float32)
    y = xf * lax.rsqrt(jnp.mean(xf * xf, axis=-1, keepdims=True) + EPS)
    return (y * g.astype(jnp.float32)).astype(x.dtype)


def layer_norm(x, g, b):
    xf = x.astype(jnp.float32)
    mu = jnp.mean(xf, axis=-1, keepdims=True)
    var = jnp.mean(jnp.square(xf - mu), axis=-1, keepdims=True)
    y = (xf - mu) * lax.rsqrt(var + EPS)
    return (y * g.astype(jnp.float32) + b.astype(jnp.float32)).astype(x.dtype)


def rope_tables(seq_len):
    inv = 1.0 / (ROPE_THETA ** (jnp.arange(0, MLA_ROPE, 2, dtype=jnp.float32) / MLA_ROPE))
    ang = jnp.arange(seq_len, dtype=jnp.float32)[:, None] * inv[None, :]
    return jnp.cos(ang), jnp.sin(ang)


def apply_rope(x, cos, sin):
    xf = x.astype(jnp.float32)
    x1, x2 = xf[..., :MLA_ROPE // 2], xf[..., MLA_ROPE // 2:]
    return jnp.concatenate([x1 * cos - x2 * sin, x2 * cos + x1 * sin], axis=-1).astype(x.dtype)


def neighborhood_attention(q, k, v, rel_bias):
    B, S, H, hd = q.shape
    rows = S // GRID_W
    kr = min(NA_ROWS, rows)
    scale = hd ** -0.5
    qg = q.reshape(B, rows, GRID_W, H, hd)
    kg = k.reshape(B, rows, GRID_W, H, hd)
    vg = v.reshape(B, rows, GRID_W, H, hd)
    r = np.arange(rows)
    rs = np.clip(r - kr // 2, 0, rows - kr)
    row_idx = rs[:, None] + np.arange(kr)[None, :]
    dr_idx = row_idx - r[:, None] + (NA_ROWS - 1)
    k_band = jnp.take(kg, row_idx, axis=1)
    v_band = jnp.take(vg, row_idx, axis=1)
    outs = []
    for c0 in range(0, GRID_W, NA_COL_BLOCK):
        start = int(np.clip(c0 - NA_COLS // 2, 0, GRID_W - NA_COL_SPAN))
        cq = np.arange(c0, c0 + NA_COL_BLOCK)
        ck = np.arange(start, start + NA_COL_SPAN)
        cs = np.clip(cq - NA_COLS // 2, 0, GRID_W - NA_COLS)
        valid = (ck[None, :] >= cs[:, None]) & (ck[None, :] < cs[:, None] + NA_COLS)
        dc_idx = np.clip(ck[None, :] - cq[:, None] + (NA_COLS - 1), 0, 2 * NA_COLS - 2)
        bias = rel_bias[:, dr_idx[:, None, :, None], dc_idx[None, :, None, :]]
        qb = qg[:, :, c0:c0 + NA_COL_BLOCK]
        kb = k_band[:, :, :, start:start + NA_COL_SPAN]
        vb = v_band[:, :, :, start:start + NA_COL_SPAN]
        s = jnp.einsum('brchd,brkwhd->bhrckw', qb, kb, preferred_element_type=jnp.float32)
        s = s * scale + bias.astype(jnp.float32)[None]
        s = jnp.where(valid[None, None, None, :, None, :], s, NEG_INF)
        shp = s.shape
        p = jax.nn.softmax(s.reshape(shp[:4] + (kr * NA_COL_SPAN,)), axis=-1).reshape(shp).astype(q.dtype)
        outs.append(jnp.einsum('bhrckw,brkwhd->brchd', p, vb))
    o = jnp.concatenate(outs, axis=2)
    return o.reshape(B, S, H * hd)


def spatial_gating(z, ln_g, ln_b, w_s, b_s):
    B, S, _ = z.shape
    z = jax.nn.gelu(z)
    u, v = z[..., :SGU_WIDTH], z[..., SGU_WIDTH:]
    v = layer_norm(v, ln_g, ln_b).reshape(B, S // SGU_CHUNK, SGU_CHUNK, SGU_GROUPS, SGU_GROUP_DIM)
    s = jnp.einsum('gpq,bnqgc->bnpgc', w_s, v) + b_s.T[None, None, :, :, None]
    return u * s.reshape(B, S, SGU_WIDTH)


def latent_attention(c_q, c_kv, k_r, cq_g, ckv_g, w_uq, w_ukv):
    B, S, _ = c_q.shape
    dt = c_q.dtype
    q = (rms_norm(c_q, cq_g) @ w_uq).reshape(B, S, MLA_HEADS, MLA_NOPE + MLA_ROPE)
    kv = (rms_norm(c_kv, ckv_g) @ w_ukv).reshape(B, S, MLA_HEADS, MLA_NOPE + MLA_V)
    q_nope, q_rope = q[..., :MLA_NOPE], q[..., MLA_NOPE:]
    k_nope, v = kv[..., :MLA_NOPE], kv[..., MLA_NOPE:]
    cos, sin = rope_tables(S)
    q_rope = apply_rope(q_rope, cos[:, None, :], sin[:, None, :])
    k_rope = apply_rope(k_r, cos, sin)
    scale = (MLA_NOPE + MLA_ROPE) ** -0.5
    nb = S // MLA_Q_BLOCK
    qn_b = q_nope.reshape(B, nb, MLA_Q_BLOCK, MLA_HEADS, MLA_NOPE).transpose(1, 0, 2, 3, 4)
    qr_b = q_rope.reshape(B, nb, MLA_Q_BLOCK, MLA_HEADS, MLA_ROPE).transpose(1, 0, 2, 3, 4)

    def attend(blk):
        qn, qr = blk
        s = (jnp.einsum('bqhd,bkhd->bhqk', qn, k_nope, preferred_element_type=jnp.float32)
             + jnp.einsum('bqhr,bkr->bhqk', qr, k_rope, preferred_element_type=jnp.float32))
        p = jax.nn.softmax(s * scale, axis=-1).astype(dt)
        return jnp.einsum('bhqk,bkhd->bqhd', p, v)

    o = lax.map(attend, (qn_b, qr_b))
    return o.transpose(1, 0, 2, 3, 4).reshape(B, S, MLA_WIDTH)


def hybrid_mixer(h, w_in, na_bias, sgu_ln_g, sgu_ln_b, sgu_w, sgu_b, cq_g, ckv_g, w_uq, w_ukv,
                 w_a_out, w_b_out, w_c_out, w_gate, b_gate, w_out):
    B, S, D = h.shape
    p = h @ w_in
    qa, ka, va, zb, c_q, c_kv, k_r = jnp.split(p, IN_SPLITS, axis=-1)
    hs = (B, S, NA_HEADS, NA_HEAD_DIM)
    o_a = neighborhood_attention(qa.reshape(hs), ka.reshape(hs), va.reshape(hs), na_bias)
    o_b = spatial_gating(zb, sgu_ln_g, sgu_ln_b, sgu_w, sgu_b)
    o_c = latent_attention(c_q, c_kv, k_r, cq_g, ckv_g, w_uq, w_ukv)
    gates = jax.nn.sigmoid((h @ w_gate + b_gate).astype(jnp.float32)).astype(h.dtype)
    gates = gates.reshape(B, S, N_BRANCH, D)
    merged = (gates[:, :, 0] * (o_a @ w_a_out) + gates[:, :, 1] * (o_b @ w_b_out)
              + gates[:, :, 2] * (o_c @ w_c_out))
    return merged @ w_out


def peer_layer(h, w_query, keys1, keys2, expert_u, expert_v):
    B, S, D = h.shape
    T = B * S
    ht = h.reshape(T, D)
    q = (ht @ w_query).reshape(T, PEER_HEADS, 2, PEER_KEY_HALF)
    s1 = jnp.einsum('thd,hnd->thn', q[:, :, 0], keys1)
    s2 = jnp.einsum('thd,hnd->thn', q[:, :, 1], keys2)
    v1, i1 = lax.top_k(s1, PEER_TOPK)
    v2, i2 = lax.top_k(s2, PEER_TOPK)
    cand = (v1[..., :, None] + v2[..., None, :]).reshape(T, PEER_HEADS, PEER_TOPK * PEER_TOPK)
    cand_idx = (i1[..., :, None] * PEER_N_KEYS + i2[..., None, :]).reshape(T, PEER_HEADS, PEER_TOPK * PEER_TOPK)
    best, pos = lax.top_k(cand, PEER_TOPK)
    idx = jnp.take_along_axis(cand_idx, pos, axis=-1)
    g = jax.nn.softmax(best.astype(jnp.float32), axis=-1).astype(h.dtype)
    nb = T // PEER_TOKEN_BLOCK
    hk = PEER_HEADS * PEER_TOPK
    xb = ht.reshape(nb, PEER_TOKEN_BLOCK, D)
    ib = idx.reshape(nb, PEER_TOKEN_BLOCK, hk)
    gb = g.reshape(nb, PEER_TOKEN_BLOCK, hk)

    def block(args):
        xt, it, gt = args
        u = jnp.take(expert_u, it, axis=0)
        a = jax.nn.gelu(jnp.einsum('td,tkd->tk', xt, u))
        vv = jnp.take(expert_v, it, axis=0)
        return jnp.einsum('tk,tkd->td', a * gt, vv)

    y = lax.map(block, (xb, ib, gb))
    return y.reshape(B, S, D)


def setup_inputs(seed: int = 0) -> dict:
    key = jax.random.key(seed)
    ks = jax.random.split(key, 32)
    f32 = jnp.float32
    nrm = lambda k, shape, s: (jax.random.normal(k, shape, f32) * s)
    L = DEPTH
    return {
        "x": nrm(ks[0], (BATCH, SEQ, D_MODEL), 1.0),
        "norm1_g": 1.0 + nrm(ks[1], (L, D_MODEL), 0.02),
        "w_in": nrm(ks[2], (L, D_MODEL, IN_COLS), D_MODEL ** -0.5),
        "na_rel_bias": nrm(ks[3], (L, NA_HEADS, 2 * NA_ROWS - 1, 2 * NA_COLS - 1), 0.1),
        "sgu_ln_g": 1.0 + nrm(ks[4], (L, SGU_WIDTH), 0.02),
        "sgu_ln_b": nrm(ks[5], (L, SGU_WIDTH), 0.02),
        "sgu_w": nrm(ks[6], (L, SGU_GROUPS, SGU_CHUNK, SGU_CHUNK), SGU_CHUNK ** -0.5),
        "sgu_b": 1.0 + nrm(ks[7], (L, SGU_GROUPS, SGU_CHUNK), 0.02),
        "mla_cq_g": 1.0 + nrm(ks[8], (L, MLA_Q_RANK), 0.02),
        "mla_ckv_g": 1.0 + nrm(ks[9], (L, MLA_KV_RANK), 0.02),
        "mla_w_uq": nrm(ks[10], (L, MLA_Q_RANK, MLA_HEADS * (MLA_NOPE + MLA_ROPE)), MLA_Q_RANK ** -0.5),
        "mla_w_ukv": nrm(ks[11], (L, MLA_KV_RANK, MLA_HEADS * (MLA_NOPE + MLA_V)), MLA_KV_RANK ** -0.5),
        "w_a_out": nrm(ks[12], (L, NA_WIDTH, D_MODEL), NA_WIDTH ** -0.5),
        "w_b_out": nrm(ks[13], (L, SGU_WIDTH, D_MODEL), SGU_WIDTH ** -0.5),
        "w_c_out": nrm(ks[14], (L, MLA_WIDTH, D_MODEL), MLA_WIDTH ** -0.5),
        "w_gate": nrm(ks[15], (L, D_MODEL, N_BRANCH * D_MODEL), D_MODEL ** -0.5),
        "b_gate": nrm(ks[16], (L, N_BRANCH * D_MODEL), 0.02),
        "w_out": nrm(ks[17], (L, D_MODEL, D_MODEL), D_MODEL ** -0.5),
        "norm2_g": 1.0 + nrm(ks[18], (L, D_MODEL), 0.02),
        "peer_w_query": nrm(ks[19], (L, D_MODEL, PEER_HEADS * 2 * PEER_KEY_HALF), D_MODEL ** -0.5),
        "peer_keys1": nrm(ks[20], (L, PEER_HEADS, PEER_N_KEYS, PEER_KEY_HALF), PEER_KEY_HALF ** -0.5),
        "peer_keys2": nrm(ks[21], (L, PEER_HEADS, PEER_N_KEYS, PEER_KEY_HALF), PEER_KEY_HALF ** -0.5),
        "peer_u": nrm(ks[22], (L, PEER_N_EXPERTS, D_MODEL), D_MODEL ** -0.5),
        "peer_v": nrm(ks[23], (L, PEER_N_EXPERTS, D_MODEL), PEER_HEADS ** -0.5),
        "final_g": 1.0 + nrm(ks[24], (D_MODEL,), 0.02),
    }


def reference(x, norm1_g, w_in, na_rel_bias, sgu_ln_g, sgu_ln_b, sgu_w, sgu_b, mla_cq_g, mla_ckv_g,
              mla_w_uq, mla_w_ukv, w_a_out, w_b_out, w_c_out, w_gate, b_gate, w_out, norm2_g,
              peer_w_query, peer_keys1, peer_keys2, peer_u, peer_v, final_g):
    for l in range(DEPTH):
        h = rms_norm(x, norm1_g[l])
        x = x + hybrid_mixer(h, w_in[l], na_rel_bias[l], sgu_ln_g[l], sgu_ln_b[l], sgu_w[l], sgu_b[l],
                             mla_cq_g[l], mla_ckv_g[l], mla_w_uq[l], mla_w_ukv[l],
                             w_a_out[l], w_b_out[l], w_c_out[l], w_gate[l], b_gate[l], w_out[l])
        h = rms_norm(x, norm2_g[l])
        x = x + peer_layer(h, peer_w_query[l], peer_keys1[l], peer_keys2[l], peer_u[l], peer_v[l])
    return rms_norm(x, final_g)
```

```python
import functools

import numpy as np
import jax
import jax.numpy as jnp
from jax import lax
from jax.experimental import pallas as pl
from jax.experimental.pallas import tpu as pltpu

D_MODEL = 2048
GRID_W = 64
EPS = 1e-6
NEG_INF = -1e30

NA_HEADS = 8
NA_HEAD_DIM = 128
NA_WIDTH = NA_HEADS * NA_HEAD_DIM
NA_ROWS = 8
NA_COLS = 16

SGU_GROUPS = 8
SGU_GROUP_DIM = 128
SGU_WIDTH = SGU_GROUPS * SGU_GROUP_DIM
SGU_CHUNK = 128

MLA_HEADS = 8
MLA_Q_RANK = 512
MLA_KV_RANK = 512
MLA_NOPE = 128
MLA_ROPE = 64
MLA_V = 128
MLA_WIDTH = MLA_HEADS * MLA_V
MLA_SLOT = 256
ROPE_THETA = 10000.0

N_BRANCH = 3

PEER_HEADS = 8
PEER_N_KEYS = 128
PEER_N_EXPERTS = PEER_N_KEYS * PEER_N_KEYS
PEER_KEY_HALF = 128
PEER_TOPK = 16

BF16 = jnp.bfloat16
F32 = jnp.float32

MIB = 1024 * 1024
NT_DIMS = (((1,), (1,)), ((), ()))
TN_DIMS = (((0,), (0,)), ((), ()))


def _params(semantics, vmem_mib):
    return pltpu.CompilerParams(dimension_semantics=semantics, vmem_limit_bytes=vmem_mib * MIB)


def _rms(x, g):
    return x * lax.rsqrt(jnp.mean(x * x, axis=-1, keepdims=True) + EPS) * g


def _rmsnorm_kernel(x_ref, g_ref, o_ref):
    o_ref[...] = _rms(x_ref[...], g_ref[...]).astype(o_ref.dtype)


def _rmsnorm(x, g, out_dtype, tm=256):
    t, d = x.shape
    return pl.pallas_call(
        _rmsnorm_kernel,
        grid=(t // tm,),
        in_specs=[pl.BlockSpec((tm, d), lambda i: (i, 0)), pl.BlockSpec((1, d), lambda i: (0, 0))],
        out_specs=pl.BlockSpec((tm, d), lambda i: (i, 0)),
        out_shape=jax.ShapeDtypeStruct((t, d), out_dtype),
        compiler_params=_params(("parallel",), 32),
        name="rmsnorm",
    )(x, g.reshape(1, d))


def _matmul_kernel(a_ref, b_ref, o_ref):
    o_ref[...] = jnp.dot(a_ref[...], b_ref[...], preferred_element_type=F32).astype(o_ref.dtype)


def _matmul(a, b, out_dtype, tm=512, tn=512, name="matmul"):
    m, k = a.shape
    n = b.shape[1]
    return pl.pallas_call(
        _matmul_kernel,
        grid=(n // tn, m // tm),
        in_specs=[pl.BlockSpec((tm, k), lambda j, i: (i, 0)), pl.BlockSpec((k, tn), lambda j, i: (0, j))],
        out_specs=pl.BlockSpec((tm, tn), lambda j, i: (i, j)),
        out_shape=jax.ShapeDtypeStruct((m, n), out_dtype),
        compiler_params=_params(("parallel", "parallel"), 32),
        name=name,
    )(a, b)


def _na_bias_table(rel_bias):
    half = NA_COLS // 2
    d0 = np.arange(NA_ROWS)
    dr_idx = d0[:, None] + np.arange(NA_ROWS)[None, :]
    cq = np.arange(GRID_W)
    ck = np.arange(GRID_W)
    cs = np.clip(cq - half, 0, GRID_W - NA_COLS)
    valid = (ck[None, :] >= cs[:, None]) & (ck[None, :] < cs[:, None] + NA_COLS)
    dc_idx = np.clip(ck[None, :] - cq[:, None] + (NA_COLS - 1), 0, 2 * NA_COLS - 2)
    bias = rel_bias[:, dr_idx[:, None, :, None], dc_idx[None, :, None, :]]
    bias = jnp.where(valid[None, None, :, None, :], bias.astype(F32), NEG_INF)
    return bias.reshape(NA_HEADS, NA_ROWS, GRID_W, NA_ROWS * GRID_W)


def _na_kernel(q_ref, k_ref, v_ref, bias_ref, o_ref, *, rows):
    scale = NA_HEAD_DIM ** -0.5
    band = NA_ROWS * GRID_W

    def row(r, carry):
        rs = jnp.clip(r - NA_ROWS // 2, 0, rows - NA_ROWS)
        d0 = rs - r + (NA_ROWS - 1)
        q0 = pl.multiple_of(r * GRID_W, GRID_W)
        k0 = pl.multiple_of(rs * GRID_W, GRID_W)
        q = q_ref[pl.ds(q0, GRID_W), :]
        kb = k_ref[pl.ds(k0, band), :]
        vb = v_ref[pl.ds(k0, band), :]
        s = lax.dot_general(q, kb, NT_DIMS, preferred_element_type=F32) * scale + bias_ref[0, d0]
        p = jnp.exp(s - jnp.max(s, axis=-1, keepdims=True))
        l = jnp.sum(p, axis=-1, keepdims=True)
        o = jnp.dot(p.astype(BF16), vb, preferred_element_type=F32) / l
        o_ref[pl.ds(q0, GRID_W), :] = o.astype(o_ref.dtype)
        return carry

    lax.fori_loop(0, rows, row, 0)


def _na_attention(qkv, bias_table, batch, seq):
    t = qkv.shape[0]
    hd = NA_HEAD_DIM
    blk = lambda off: pl.BlockSpec((seq, hd), lambda b, h: (b, off + h))
    return pl.pallas_call(
        functools.partial(_na_kernel, rows=seq // GRID_W),
        grid=(batch, NA_HEADS),
        in_specs=[blk(0), blk(NA_HEADS), blk(2 * NA_HEADS),
                  pl.BlockSpec((1, NA_ROWS, GRID_W, NA_ROWS * GRID_W), lambda b, h: (h, 0, 0, 0))],
        out_specs=pl.BlockSpec((seq, hd), lambda b, h: (b, h)),
        out_shape=jax.ShapeDtypeStruct((t, NA_WIDTH), BF16),
        compiler_params=_params(("parallel", "parallel"), 32),
        name="na_attention",
    )(qkv, qkv, qkv, bias_table)


def _sgu_kernel(zu_ref, zv_ref, g_ref, b_ref, w_ref, bs_ref, o_ref, *, chunks):
    u = jax.nn.gelu(zu_ref[...])
    v = jax.nn.gelu(zv_ref[...])
    mu = jnp.mean(v, axis=-1, keepdims=True)
    vc = v - mu
    var = jnp.mean(vc * vc, axis=-1, keepdims=True)
    vn = (vc * lax.rsqrt(var + EPS) * g_ref[...] + b_ref[...]).astype(BF16)
    for n in range(chunks):
        rows = slice(n * SGU_CHUNK, (n + 1) * SGU_CHUNK)
        for g in range(SGU_GROUPS):
            cols = slice(g * SGU_GROUP_DIM, (g + 1) * SGU_GROUP_DIM)
            s = jnp.dot(w_ref[g], vn[rows, cols], preferred_element_type=F32) + bs_ref[g]
            o_ref[rows, cols] = (u[rows, cols] * s).astype(o_ref.dtype)


def _sgu(z, ln_g, ln_b, w_s, b_s, chunks=4):
    t = z.shape[0]
    tb = chunks * SGU_CHUNK
    bs = jnp.broadcast_to(b_s.astype(F32)[:, :, None], (SGU_GROUPS, SGU_CHUNK, SGU_GROUP_DIM))
    const3 = lambda i: (0, 0, 0)
    return pl.pallas_call(
        functools.partial(_sgu_kernel, chunks=chunks),
        grid=(t // tb,),
        in_specs=[pl.BlockSpec((tb, SGU_WIDTH), lambda i: (i, 0)),
                  pl.BlockSpec((tb, SGU_WIDTH), lambda i: (i, 1)),
                  pl.BlockSpec((1, SGU_WIDTH), lambda i: (0, 0)),
                  pl.BlockSpec((1, SGU_WIDTH), lambda i: (0, 0)),
                  pl.BlockSpec((SGU_GROUPS, SGU_CHUNK, SGU_CHUNK), const3),
                  pl.BlockSpec((SGU_GROUPS, SGU_CHUNK, SGU_GROUP_DIM), const3)],
        out_specs=pl.BlockSpec((tb, SGU_WIDTH), lambda i: (i, 0)),
        out_shape=jax.ShapeDtypeStruct((t, SGU_WIDTH), BF16),
        compiler_params=_params(("parallel",), 32),
        name="sgu",
    )(z, z, ln_g.reshape(1, -1), ln_b.reshape(1, -1), w_s.astype(BF16), bs)


def _rope_slot_tables(seq):
    inv = 1.0 / (ROPE_THETA ** (jnp.arange(0, MLA_ROPE, 2, dtype=F32) / MLA_ROPE))
    ang = jnp.arange(seq, dtype=F32)[:, None] * inv[None, :]
    cos, sin = jnp.cos(ang), jnp.sin(ang)
    pad = jnp.zeros((seq, MLA_SLOT - MLA_NOPE - MLA_ROPE), F32)
    cos_t = jnp.concatenate([jnp.ones((seq, MLA_NOPE), F32), cos, cos, pad], axis=1)
    sin_t = jnp.concatenate([jnp.zeros((seq, MLA_NOPE), F32), sin, sin, pad], axis=1)
    return cos_t, sin_t


def _rotate_half_cols(w):
    half = MLA_ROPE // 2
    return jnp.concatenate([-w[..., half:], w[..., :half]], axis=-1)


def _mla_weights(w_uq, w_ukv, w_kr):
    rq = w_uq.shape[0]
    wq = w_uq.reshape(rq, MLA_HEADS, MLA_NOPE + MLA_ROPE)
    nope, rope = wq[..., :MLA_NOPE], wq[..., MLA_NOPE:]
    zpad = jnp.zeros((rq, MLA_HEADS, MLA_SLOT - MLA_NOPE - MLA_ROPE), w_uq.dtype)
    wq_slot = jnp.concatenate([nope, rope, zpad], axis=2).reshape(rq, MLA_HEADS * MLA_SLOT)
    wq_rot = jnp.concatenate([jnp.zeros_like(nope), _rotate_half_cols(rope), zpad], axis=2)
    wq_rot = wq_rot.reshape(rq, MLA_HEADS * MLA_SLOT)
    rkv = w_ukv.shape[0]
    wkv = w_ukv.reshape(rkv, MLA_HEADS, MLA_NOPE + MLA_V)
    k_nope, v = wkv[..., :MLA_NOPE], wkv[..., MLA_NOPE:]
    wk_slot = jnp.concatenate([k_nope, jnp.zeros_like(k_nope)], axis=2).reshape(rkv, MLA_HEADS * MLA_SLOT)
    wv = v.reshape(rkv, MLA_WIDTH)
    d = w_kr.shape[0]
    lead = jnp.zeros((d, MLA_NOPE), w_kr.dtype)
    tail = jnp.zeros((d, MLA_SLOT - MLA_NOPE - MLA_ROPE), w_kr.dtype)
    wkr_slot = jnp.concatenate([lead, w_kr, tail], axis=1)
    wkr_rot = jnp.concatenate([lead, _rotate_half_cols(w_kr), tail], axis=1)
    return wq_slot, wq_rot, wk_slot, wv, wkr_slot, wkr_rot


def _mla_prep_kernel(cq_ref, ckv_ref, kr_ref, krr_ref, cqg_ref, ckvg_ref, cos_ref, sin_ref,
                     wq_ref, wqr_ref, wk_ref, wv_ref, q_out, k_out, v_out):
    scale = (MLA_NOPE + MLA_ROPE) ** -0.5
    cq = _rms(cq_ref[...], cqg_ref[...]).astype(BF16)
    ckv = _rms(ckv_ref[...], ckvg_ref[...]).astype(BF16)
    cos = cos_ref[...]
    sin = sin_ref[...]
    k_rope = kr_ref[...] * cos + krr_ref[...] * sin
    for h in range(MLA_HEADS):
        sl = slice(h * MLA_SLOT, (h + 1) * MLA_SLOT)
        qh = (jnp.dot(cq, wq_ref[:, sl], preferred_element_type=F32) * cos
              + jnp.dot(cq, wqr_ref[:, sl], preferred_element_type=F32) * sin)
        q_out[:, sl] = (qh * scale).astype(q_out.dtype)
        kh = jnp.dot(ckv, wk_ref[:, sl], preferred_element_type=F32) + k_rope
        k_out[:, sl] = kh.astype(k_out.dtype)
    v_out[...] = jnp.dot(ckv, wv_ref[...], preferred_element_type=F32).astype(v_out.dtype)


def _mla_prep(c, cq_g, ckv_g, cos_t, sin_t, wq_slot, wq_rot, wk_slot, wv, seq, tm=256):
    t = c.shape[0]
    pos_blocks = seq // tm
    qk_w = MLA_HEADS * MLA_SLOT
    const = lambda i: (0, 0)
    return pl.pallas_call(
        _mla_prep_kernel,
        grid=(t // tm,),
        in_specs=[pl.BlockSpec((tm, MLA_Q_RANK), lambda i: (i, 0)),
                  pl.BlockSpec((tm, MLA_KV_RANK), lambda i: (i, 1)),
                  pl.BlockSpec((tm, MLA_SLOT), lambda i: (i, 4)),
                  pl.BlockSpec((tm, MLA_SLOT), lambda i: (i, 5)),
                  pl.BlockSpec((1, MLA_Q_RANK), const),
                  pl.BlockSpec((1, MLA_KV_RANK), const),
                  pl.BlockSpec((tm, MLA_SLOT), lambda i: (i % pos_blocks, 0)),
                  pl.BlockSpec((tm, MLA_SLOT), lambda i: (i % pos_blocks, 0)),
                  pl.BlockSpec((MLA_Q_RANK, qk_w), const),
                  pl.BlockSpec((MLA_Q_RANK, qk_w), const),
                  pl.BlockSpec((MLA_KV_RANK, qk_w), const),
                  pl.BlockSpec((MLA_KV_RANK, MLA_WIDTH), const)],
        out_specs=[pl.BlockSpec((tm, qk_w), lambda i: (i, 0)),
                   pl.BlockSpec((tm, qk_w), lambda i: (i, 0)),
                   pl.BlockSpec((tm, MLA_WIDTH), lambda i: (i, 0))],
        out_shape=[jax.ShapeDtypeStruct((t, qk_w), BF16),
                   jax.ShapeDtypeStruct((t, qk_w), BF16),
                   jax.ShapeDtypeStruct((t, MLA_WIDTH), BF16)],
        compiler_params=_params(("parallel",), 48),
        name="mla_prep",
    )(c, c, c, c, cq_g.reshape(1, -1), ckv_g.reshape(1, -1), cos_t, sin_t, wq_slot, wq_rot, wk_slot, wv)


def _mla_attn_kernel(q_ref, k_ref, v_ref, o_ref):
    s = lax.dot_general(q_ref[...], k_ref[...], NT_DIMS, preferred_element_type=F32)
    p = jnp.exp(s - jnp.max(s, axis=-1, keepdims=True))
    l = jnp.sum(p, axis=-1, keepdims=True)
    o = jnp.dot(p.astype(BF16), v_ref[...], preferred_element_type=F32) / l
    o_ref[...] = o.astype(o_ref.dtype)


def _mla_attention(q, k, v, batch, seq, tq=256):
    t = q.shape[0]
    qb = seq // tq
    return pl.pallas_call(
        _mla_attn_kernel,
        grid=(batch, MLA_HEADS, qb),
        in_specs=[pl.BlockSpec((tq, MLA_SLOT), lambda b, h, i: (b * qb + i, h)),
                  pl.BlockSpec((seq, MLA_SLOT), lambda b, h, i: (b, h)),
                  pl.BlockSpec((seq, MLA_V), lambda b, h, i: (b, h))],
        out_specs=pl.BlockSpec((tq, MLA_V), lambda b, h, i: (b * qb + i, h)),
        out_shape=jax.ShapeDtypeStruct((t, MLA_WIDTH), BF16),
        compiler_params=_params(("parallel", "parallel", "parallel"), 48),
        name="mla_attention",
    )(q, k, v)


def _gate_merge_kernel(h_ref, oa_ref, ob_ref, oc_ref, wg0_ref, wg1_ref, wg2_ref, bg0_ref, bg1_ref, bg2_ref,
                       wa_ref, wb_ref, wc_ref, o_ref):
    h = h_ref[...]
    acc = None
    for o_br, wg, bg, w in ((oa_ref, wg0_ref, bg0_ref, wa_ref), (ob_ref, wg1_ref, bg1_ref, wb_ref),
                            (oc_ref, wg2_ref, bg2_ref, wc_ref)):
        gate = jax.nn.sigmoid(jnp.dot(h, wg[...], preferred_element_type=F32) + bg[...])
        term = gate * jnp.dot(o_br[...], w[...], preferred_element_type=F32)
        acc = term if acc is None else acc + term
    o_ref[...] = acc.astype(o_ref.dtype)


def _gate_merge(h, o_a, o_b, o_c, w_gate, b_gate, w_a, w_b, w_c, tm=512, tn=512):
    t, d = h.shape
    nb = d // tn
    b_gate = b_gate.reshape(1, -1)
    row = lambda width: pl.BlockSpec((tm, width), lambda j, i: (i, 0))
    gate_w = lambda br: pl.BlockSpec((d, tn), lambda j, i: (0, br * nb + j))
    gate_b = lambda br: pl.BlockSpec((1, tn), lambda j, i: (0, br * nb + j))
    proj_w = lambda width: pl.BlockSpec((width, tn), lambda j, i: (0, j))
    return pl.pallas_call(
        _gate_merge_kernel,
        grid=(nb, t // tm),
        in_specs=[row(d), row(NA_WIDTH), row(SGU_WIDTH), row(MLA_WIDTH),
                  gate_w(0), gate_w(1), gate_w(2), gate_b(0), gate_b(1), gate_b(2),
                  proj_w(NA_WIDTH), proj_w(SGU_WIDTH), proj_w(MLA_WIDTH)],
        out_specs=pl.BlockSpec((tm, tn), lambda j, i: (i, j)),
        out_shape=jax.ShapeDtypeStruct((t, d), BF16),
        compiler_params=_params(("parallel", "parallel"), 48),
        name="gate_merge",
    )(h, o_a, o_b, o_c, w_gate, w_gate, w_gate, b_gate, b_gate, b_gate, w_a, w_b, w_c)


def _out_proj_kernel(m_ref, w_ref, x_ref, g_ref, xo_ref, ho_ref):
    xn = x_ref[...] + jnp.dot(m_ref[...], w_ref[...], preferred_element_type=F32)
    xo_ref[...] = xn
    ho_ref[...] = _rms(xn, g_ref[...]).astype(ho_ref.dtype)


def _out_proj_residual_norm(merged, w_out, x, g, tm=256):
    t, d = x.shape
    row = pl.BlockSpec((tm, d), lambda i: (i, 0))
    return pl.pallas_call(
        _out_proj_kernel,
        grid=(t // tm,),
        in_specs=[row, pl.BlockSpec((d, d), lambda i: (0, 0)), row, pl.BlockSpec((1, d), lambda i: (0, 0))],
        out_specs=[row, row],
        out_shape=[jax.ShapeDtypeStruct((t, d), F32), jax.ShapeDtypeStruct((t, d), BF16)],
        compiler_params=_params(("parallel",), 48),
        name="out_proj",
    )(merged, w_out, x, g.reshape(1, d))


def _top_values(x, iota, k):
    n = x.shape[0]
    vals = []
    for _ in range(k):
        m = jnp.max(x, axis=0, keepdims=True)
        vals.append(m)
        first = jnp.min(jnp.where(x == m, iota, n), axis=0, keepdims=True)
        x = jnp.where(iota == first, -jnp.inf, x)
    return vals


def _peer_route_kernel(h_ref, wq_ref, k1_ref, k2_ref, s1_ref, s2_ref, e1_ref, e2_ref, tau_ref):
    tm = h_ref.shape[0]
    q = jnp.dot(h_ref[...], wq_ref[...], preferred_element_type=F32)
    iota_keys = lax.broadcasted_iota(jnp.int32, (PEER_N_KEYS, tm), 0)
    iota_cand = lax.broadcasted_iota(jnp.int32, (PEER_TOPK * PEER_TOPK, tm), 0)
    for h in range(PEER_HEADS):
        c0 = h * 2 * PEER_KEY_HALF
        q1 = q[:, c0:c0 + PEER_KEY_HALF].astype(BF16)
        q2 = q[:, c0 + PEER_KEY_HALF:c0 + 2 * PEER_KEY_HALF].astype(BF16)
        s1 = lax.dot_general(k1_ref[h], q1, NT_DIMS, preferred_element_type=F32)
        s2 = lax.dot_general(k2_ref[h], q2, NT_DIMS, preferred_element_type=F32)
        v1 = _top_values(s1, iota_keys, PEER_TOPK)
        v2 = jnp.concatenate(_top_values(s2, iota_keys, PEER_TOPK), axis=0)
        cand = jnp.concatenate([v1[a] + v2 for a in range(PEER_TOPK)], axis=0)
        best = _top_values(cand, iota_cand, PEER_TOPK)
        z = sum(jnp.exp(b - best[0]) for b in best)
        s1_ref[h] = s1
        s2_ref[h] = s2
        e1_ref[h] = jnp.exp(s1 - v1[0])
        e2_ref[h] = jnp.exp(s2 - v2[0:1]) / z
        tau_ref[h:h + 1, :] = best[PEER_TOPK - 1]


def _peer_route(h2, w_query, keys1, keys2, tm=256):
    t, d = h2.shape
    nk = PEER_N_KEYS
    score = pl.BlockSpec((PEER_HEADS, nk, tm), lambda i: (0, 0, i))
    score_shape = jax.ShapeDtypeStruct((PEER_HEADS, nk, t), F32)
    keys = pl.BlockSpec((PEER_HEADS, nk, PEER_KEY_HALF), lambda i: (0, 0, 0))
    return pl.pallas_call(
        _peer_route_kernel,
        grid=(t // tm,),
        in_specs=[pl.BlockSpec((tm, d), lambda i: (i, 0)),
                  pl.BlockSpec((d, w_query.shape[1]), lambda i: (0, 0)), keys, keys],
        out_specs=[score, score, score, score, pl.BlockSpec((PEER_HEADS, tm), lambda i: (0, i))],
        out_shape=[score_shape] * 4 + [jax.ShapeDtypeStruct((PEER_HEADS, t), F32)],
        compiler_params=_params(("parallel",), 48),
        name="peer_route",
    )(h2, w_query, keys1, keys2)


def _peer_expert_kernel(h_ref, s1_ref, s2_ref, e1_ref, e2_ref, tau_ref, u_ref, v_ref, x_ref, g_ref,
                        xo_ref, hn_ref, gw_ref, *, final):
    j = pl.program_id(1)
    ec = u_ref.shape[0]
    groups = ec // PEER_N_KEYS

    @pl.when(j == 0)
    def _():
        xo_ref[...] = x_ref[...]

    a = lax.dot_general(u_ref[...], h_ref[...], NT_DIMS, preferred_element_type=F32)
    act = jax.nn.gelu(a)
    for gi in range(groups):
        i = j * groups + gi
        w = None
        for h in range(PEER_HEADS):
            cand = s1_ref[h, pl.ds(i, 1), :] + s2_ref[h]
            sel = jnp.where(cand >= tau_ref[h:h + 1, :], e2_ref[h], 0.0) * e1_ref[h, pl.ds(i, 1), :]
            w = sel if w is None else w + sel
        rows = slice(gi * PEER_N_KEYS, (gi + 1) * PEER_N_KEYS)
        gw_ref[rows, :] = (act[rows, :] * w).astype(gw_ref.dtype)
    xo_ref[...] += lax.dot_general(gw_ref[...], v_ref[...], TN_DIMS, preferred_element_type=F32)

    @pl.when(j == pl.num_programs(1) - 1)
    def _():
        normed = _rms(xo_ref[...], g_ref[...])
        if final:
            xo_ref[...] = normed
        hn_ref[...] = normed.astype(hn_ref.dtype)


def _peer_experts(h2, route, u, v, x, g_next, final, tb=256, ec=512):
    s1, s2, e1, e2, tau = route
    t, d = x.shape
    nk = PEER_N_KEYS
    score = pl.BlockSpec((PEER_HEADS, nk, tb), lambda i, j: (0, 0, i))
    row = pl.BlockSpec((tb, d), lambda i, j: (i, 0))
    expert = pl.BlockSpec((ec, d), lambda i, j: (j, 0))
    return pl.pallas_call(
        functools.partial(_peer_expert_kernel, final=final),
        grid=(t // tb, PEER_N_EXPERTS // ec),
        in_specs=[row, score, score, score, score, pl.BlockSpec((PEER_HEADS, tb), lambda i, j: (0, i)),
                  expert, expert, row, pl.BlockSpec((1, d), lambda i, j: (0, 0))],
        out_specs=[row, row],
        out_shape=[jax.ShapeDtypeStruct((t, d), F32), jax.ShapeDtypeStruct((t, d), BF16)],
        scratch_shapes=[pltpu.VMEM((ec, tb), BF16)],
        compiler_params=_params(("parallel", "arbitrary"), 56),
        name="peer_experts",
    )(h2, s1, s2, e1, e2, tau, u, v, x, g_next.reshape(1, d))


def kernel(x, norm1_g, w_in, na_rel_bias, sgu_ln_g, sgu_ln_b, sgu_w, sgu_b, mla_cq_g, mla_ckv_g, mla_w_uq, mla_w_ukv, w_a_out, w_b_out, w_c_out, w_gate, b_gate, w_out, norm2_g, peer_w_query, peer_keys1, peer_keys2, peer_u, peer_v, final_g):
    batch, seq, d = x.shape
    depth = norm1_g.shape[0]
    xt = x.reshape(batch * seq, d)
    cos_t, sin_t = _rope_slot_tables(seq)
    qkv_cols = 3 * NA_WIDTH
    z_cols = qkv_cols + 2 * SGU_WIDTH
    kr_col = z_cols + MLA_Q_RANK + MLA_KV_RANK

    h = _rmsnorm(xt, norm1_g[0], BF16)
    for l in range(depth):
        wl = w_in[l]
        wq_slot, wq_rot, wk_slot, wv, wkr_slot, wkr_rot = _mla_weights(mla_w_uq[l], mla_w_ukv[l], wl[:, kr_col:])
        w_c = jnp.concatenate([wl[:, z_cols:kr_col], wkr_slot, wkr_rot], axis=1).astype(BF16)

        qkv = _matmul(h, wl[:, :qkv_cols].astype(BF16), BF16, name="proj_qkv")
        z = _matmul(h, wl[:, qkv_cols:z_cols].astype(BF16), F32, name="proj_sgu")
        c = _matmul(h, w_c, F32, name="proj_mla")

        o_a = _na_attention(qkv, _na_bias_table(na_rel_bias[l]), batch, seq)
        o_b = _sgu(z, sgu_ln_g[l], sgu_ln_b[l], sgu_w[l], sgu_b[l])
        q_m, k_m, v_m = _mla_prep(c, mla_cq_g[l], mla_ckv_g[l], cos_t, sin_t, wq_slot.astype(BF16),
                                  wq_rot.astype(BF16), wk_slot.astype(BF16), wv.astype(BF16), seq)
        o_c = _mla_attention(q_m, k_m, v_m, batch, seq)

        merged = _gate_merge(h, o_a, o_b, o_c, w_gate[l].astype(BF16), b_gate[l], w_a_out[l].astype(BF16),
                             w_b_out[l].astype(BF16), w_c_out[l].astype(BF16))
        xt, h2 = _out_proj_residual_norm(merged, w_out[l].astype(BF16), xt, norm2_g[l])

        route = _peer_route(h2, peer_w_query[l].astype(BF16), peer_keys1[l].astype(BF16),
                            peer_keys2[l].astype(BF16))
        final = l == depth - 1
        g_next = final_g if final else norm1_g[l + 1]
        xt, h = _peer_experts(h2, route, peer_u[l].astype(BF16), peer_v[l].astype(BF16), xt, g_next, final)
    return xt.reshape(batch, seq, d)
```
